```python
import math
import jax, jax.numpy as jnp
from jax import lax
import numpy as np

D_MODEL = 1024
BATCH = 4
SEQ = 4096
DEPTH = 4

N_MIXERS = 2
N_MLA = (DEPTH + 1) // 2
N_SSD = DEPTH // 2
N_SUB = 3
EPS = 1e-6

D_FF = 2816

MLA_HEADS = 16
Q_LORA = 384
KV_LORA = 256
QK_NOPE = 64
QK_ROPE = 32
QK_HEAD = QK_NOPE + QK_ROPE
V_HEAD = 64
MLA_A_DIM = Q_LORA + KV_LORA + QK_ROPE
ROPE_THETA = 10000.0
Q_BLOCK = 128
MAX_POS_OFFSET = 1024

SSD_EXPAND = 2
D_INNER = SSD_EXPAND * D_MODEL
SSD_HEAD_DIM = 64
SSD_HEADS = D_INNER // SSD_HEAD_DIM
SSD_GROUPS = 4
SSD_STATE = 128
CONV_WIDTH = 4
CHUNK = 128
CONV_DIM = D_INNER + 2 * SSD_GROUPS * SSD_STATE
IN_PROJ_DIM = 2 * D_INNER + 2 * SSD_GROUPS * SSD_STATE + SSD_HEADS
DT_MIN = 0.001
DT_MAX = 0.1

kernel_name = 'hybrid_mla_ssd_macaron_adaln'


def rms_norm(x, gain):
    xf = x.astype(jnp.float32)
    y = xf * lax.rsqrt(jnp.mean(xf * xf, axis=-1, keepdims=True) + EPS)
    return (y * gain.astype(jnp.float32)).astype(x.dtype)


def modulate(x, gain, mod):
    return rms_norm(x, gain) * (1 + mod[:, None, 1]) + mod[:, None, 0]


def swiglu(h, w_gu, w_down):
    g, u = jnp.split(h @ w_gu, 2, axis=-1)
    return (jax.nn.silu(g) * u) @ w_down


def rope_tables(positions):
    inv = 1.0 / (ROPE_THETA ** (jnp.arange(0, QK_ROPE, 2, dtype=jnp.float32) / QK_ROPE))
    ang = positions.astype(jnp.float32)[..., None] * inv
    return jnp.cos(ang), jnp.sin(ang)


def apply_rope(x, cos, sin):
    x1, x2 = jnp.split(x, 2, axis=-1)
    cos = cos[:, :, None].astype(x.dtype)
    sin = sin[:, :, None].astype(x.dtype)
    return jnp.concatenate([x1 * cos - x2 * sin, x1 * sin + x2 * cos], axis=-1)


def causal_block_attention(q, k, v):
    Bn, S, H, Dh = q.shape
    Dv = v.shape[-1]
    nb = S // Q_BLOCK
    scale = Dh ** -0.5
    qb = q.reshape(Bn, nb, Q_BLOCK, H, Dh).transpose(1, 0, 3, 2, 4)
    kt = k.transpose(0, 2, 1, 3)
    vt = v.transpose(0, 2, 1, 3)
    k_pos = jnp.arange(S)

    def one_block(args):
        q_blk, i = args
        s = jnp.einsum('bhqd,bhkd->bhqk', q_blk, kt, preferred_element_type=jnp.float32) * scale
        q_pos = i * Q_BLOCK + jnp.arange(Q_BLOCK)
        s = jnp.where(k_pos[None, :] <= q_pos[:, None], s, -jnp.inf)
        p = jax.nn.softmax(s, axis=-1).astype(vt.dtype)
        return jnp.einsum('bhqk,bhkd->bhqd', p, vt)

    o = lax.map(one_block, (qb, jnp.arange(nb)))
    return o.transpose(1, 0, 3, 2, 4).reshape(Bn, S, H, Dv)


def mla_mixer(h, cos, sin, w_a, q_a_gain, kv_a_gain, w_qb, w_kvb, q_gain, k_gain, w_o):
    Bn, S, _ = h.shape
    q_lat, kv_lat, k_rope = jnp.split(h @ w_a, [Q_LORA, Q_LORA + KV_LORA], axis=-1)
    q = (rms_norm(q_lat, q_a_gain) @ w_qb).reshape(Bn, S, MLA_HEADS, QK_HEAD)
    kv = (rms_norm(kv_lat, kv_a_gain) @ w_kvb).reshape(Bn, S, MLA_HEADS, QK_NOPE + V_HEAD)
    k_nope, v = jnp.split(kv, [QK_NOPE], axis=-1)
    k_rope = jnp.broadcast_to(k_rope[:, :, None, :], (Bn, S, MLA_HEADS, QK_ROPE))
    k = jnp.concatenate([k_nope, k_rope], axis=-1)
    q = rms_norm(q, q_gain)
    k = rms_norm(k, k_gain)
    q = jnp.concatenate([q[..., :QK_NOPE], apply_rope(q[..., QK_NOPE:], cos, sin)], axis=-1)
    k = jnp.concatenate([k[..., :QK_NOPE], apply_rope(k[..., QK_NOPE:], cos, sin)], axis=-1)
    o = causal_block_attention(q, k, v)
    return o.reshape(Bn, S, MLA_HEADS * V_HEAD) @ w_o


def causal_depthwise_conv(u, w, b):
    out = lax.conv_general_dilated(
        u, w[:, None, :].astype(u.dtype), window_strides=(1,), padding=[(CONV_WIDTH - 1, 0)],
        dimension_numbers=('NWC', 'WIO', 'NWC'), feature_group_count=u.shape[-1])
    return out + b


def ssd_chunked_scan(x, dt, A, Bm, Cm):
    Bn, S, H, P = x.shape
    G, N = Bm.shape[2], Bm.shape[3]
    K = H // G
    nc = S // CHUNK
    f32 = jnp.float32
    xdt = (x.astype(f32) * dt[..., None]).reshape(Bn, nc, CHUNK, G, K, P)
    a = (dt * A).reshape(Bn, nc, CHUNK, G, K).transpose(0, 1, 3, 4, 2)
    Bc = Bm.astype(f32).reshape(Bn, nc, CHUNK, G, N)
    Cc = Cm.astype(f32).reshape(Bn, nc, CHUNK, G, N)
    a_cum = jnp.cumsum(a, axis=-1)
    seg = a_cum[..., :, None] - a_cum[..., None, :]
    causal = jnp.tril(jnp.ones((CHUNK, CHUNK), dtype=bool))
    decay_ls = jnp.exp(jnp.where(causal, seg, -jnp.inf))
    cb = jnp.einsum('bclgn,bcsgn->bcgls', Cc, Bc)
    y_diag = jnp.einsum('bcgls,bcgkls,bcsgkp->bclgkp', cb, decay_ls, xdt)
    decay_to_end = jnp.exp(a_cum[..., -1:] - a_cum)
    states = jnp.einsum('bclgn,bcgkl,bclgkp->bcgkpn', Bc, decay_to_end, xdt)
    chunk_decay = jnp.exp(a_cum[..., -1])

    def step(carry, inp):
        st, dec = inp
        return carry * dec[..., None, None] + st, carry

    init = jnp.zeros((Bn, G, K, P, N), f32)
    _, prev = lax.scan(step, init, (states.transpose(1, 0, 2, 3, 4, 5), chunk_decay.transpose(1, 0, 2, 3)))
    prev = prev.transpose(1, 0, 2, 3, 4, 5)
    y_off = jnp.einsum('bclgn,bcgkpn,bcgkl->bclgkp', Cc, prev, jnp.exp(a_cum))
    return (y_diag + y_off).reshape(Bn, S, H, P).astype(x.dtype)


def ssd_mixer(h, w_in, conv_w, conv_b, dt_bias, a_log, d_skip, norm_gain, w_out):
    Bn, S, _ = h.shape
    z, xbc, dt = jnp.split(h @ w_in, [D_INNER, D_INNER + CONV_DIM], axis=-1)
    xbc = jax.nn.silu(causal_depthwise_conv(xbc, conv_w, conv_b))
    xs, Bm, Cm = jnp.split(xbc, [D_INNER, D_INNER + SSD_GROUPS * SSD_STATE], axis=-1)
    xs = xs.reshape(Bn, S, SSD_HEADS, SSD_HEAD_DIM)
    Bm = Bm.reshape(Bn, S, SSD_GROUPS, SSD_STATE)
    Cm = Cm.reshape(Bn, S, SSD_GROUPS, SSD_STATE)
    dt = jax.nn.softplus(dt.astype(jnp.float32) + dt_bias.astype(jnp.float32))
    A = -jnp.exp(a_log.astype(jnp.float32))
    y = ssd_chunked_scan(xs, dt, A, Bm, Cm)
    y = (y + d_skip[:, None] * xs).reshape(Bn, S, D_INNER)
    g = (y * jax.nn.silu(z)).reshape(Bn, S, SSD_GROUPS, D_INNER // SSD_GROUPS)
    g = rms_norm(g, norm_gain.reshape(SSD_GROUPS, D_INNER // SSD_GROUPS)).reshape(Bn, S, D_INNER)
    return g @ w_out


def setup_inputs(seed: int = 0) -> dict:
    key = jax.random.key(seed)
    ks = jax.random.split(key, 32)
    f32 = jnp.float32

    def nrm(k, shape, fan_in, mult=1.0):
        return jax.random.normal(k, shape, f32) * (mult * fan_in ** -0.5)

    def gain(k, shape):
        return 1.0 + 0.02 * jax.random.normal(k, shape, f32)

    x = jax.random.normal(ks[0], (BATCH, SEQ, D_MODEL), f32)
    c = jax.random.normal(ks[1], (BATCH, D_MODEL), f32)
    positions = (jax.random.randint(ks[2], (BATCH, 1), 0, MAX_POS_OFFSET, dtype=jnp.int32)
                 + jnp.arange(SEQ, dtype=jnp.int32)[None, :])
    norm_gain = gain(ks[3], (DEPTH, N_SUB, D_MODEL))
    ada_w = nrm(ks[4], (DEPTH, D_MODEL, N_SUB * 3 * D_MODEL), D_MODEL, 0.5)
    ada_b = 0.02 * jax.random.normal(ks[5], (DEPTH, N_SUB * 3 * D_MODEL), f32)
    ffn_w_gu = nrm(ks[6], (DEPTH, 2, D_MODEL, 2 * D_FF), D_MODEL)
    ffn_w_down = nrm(ks[7], (DEPTH, 2, D_FF, D_MODEL), D_FF)
    mla_w_a = nrm(ks[8], (N_MLA, D_MODEL, MLA_A_DIM), D_MODEL)
    mla_q_a_gain = gain(ks[9], (N_MLA, Q_LORA))
    mla_kv_a_gain = gain(ks[10], (N_MLA, KV_LORA))
    mla_w_qb = nrm(ks[11], (N_MLA, Q_LORA, MLA_HEADS * QK_HEAD), Q_LORA)
    mla_w_kvb = nrm(ks[12], (N_MLA, KV_LORA, MLA_HEADS * (QK_NOPE + V_HEAD)), KV_LORA)
    mla_q_gain = gain(ks[13], (N_MLA, QK_HEAD))
    mla_k_gain = gain(ks[14], (N_MLA, QK_HEAD))
    mla_w_o = nrm(ks[15], (N_MLA, MLA_HEADS * V_HEAD, D_MODEL), MLA_HEADS * V_HEAD)
    ssd_w_in = nrm(ks[16], (N_SSD, D_MODEL, IN_PROJ_DIM), D_MODEL)
    ssd_conv_w = nrm(ks[17], (N_SSD, CONV_WIDTH, CONV_DIM), CONV_WIDTH)
    ssd_conv_b = 0.02 * jax.random.normal(ks[18], (N_SSD, CONV_DIM), f32)
    dt0 = jnp.exp(jax.random.uniform(ks[19], (N_SSD, SSD_HEADS), f32, math.log(DT_MIN), math.log(DT_MAX)))
    ssd_dt_bias = dt0 + jnp.log(-jnp.expm1(-dt0))
    ssd_a_log = jnp.log(jax.random.uniform(ks[20], (N_SSD, SSD_HEADS), f32, 1.0, 16.0))
    ssd_d = gain(ks[21], (N_SSD, SSD_HEADS))
    ssd_norm_gain = gain(ks[22], (N_SSD, D_INNER))
    ssd_w_out = nrm(ks[23], (N_SSD, D_INNER, D_MODEL), D_INNER)
    return {
        'x': x, 'c': c, 'positions': positions,
        'norm_gain': norm_gain, 'ada_w': ada_w, 'ada_b': ada_b,
        'ffn_w_gu': ffn_w_gu, 'ffn_w_down': ffn_w_down,
        'mla_w_a': mla_w_a, 'mla_q_a_gain': mla_q_a_gain, 'mla_kv_a_gain': mla_kv_a_gain,
        'mla_w_qb': mla_w_qb, 'mla_w_kvb': mla_w_kvb, 'mla_q_gain': mla_q_gain,
        'mla_k_gain': mla_k_gain, 'mla_w_o': mla_w_o,
        'ssd_w_in': ssd_w_in, 'ssd_conv_w': ssd_conv_w, 'ssd_conv_b': ssd_conv_b,
        'ssd_dt_bias': ssd_dt_bias, 'ssd_a_log': ssd_a_log, 'ssd_d': ssd_d,
        'ssd_norm_gain': ssd_norm_gain, 'ssd_w_out': ssd_w_out,
    }


def reference(x, c, positions, norm_gain, ada_w, ada_b, ffn_w_gu, ffn_w_down,
              mla_w_a, mla_q_a_gain, mla_kv_a_gain, mla_w_qb, mla_w_kvb, mla_q_gain,
              mla_k_gain, mla_w_o, ssd_w_in, ssd_conv_w, ssd_conv_b, ssd_dt_bias,
              ssd_a_log, ssd_d, ssd_norm_gain, ssd_w_out):
    Bn = x.shape[0]
    cos, sin = rope_tables(positions)
    mods = jnp.einsum('bd,lde->lbe', jax.nn.silu(c), ada_w) + ada_b[:, None, :]
    mods = mods.reshape(DEPTH, Bn, N_SUB, 3, D_MODEL)
    for i in range(DEPTH):
        m = mods[i]
        j = i // N_MIXERS
        h = modulate(x, norm_gain[i, 0], m[:, 0])
        x = x + 0.5 * m[:, None, 0, 2] * swiglu(h, ffn_w_gu[i, 0], ffn_w_down[i, 0])
        h = modulate(x, norm_gain[i, 1], m[:, 1])
        if i % N_MIXERS == 0:
            y = mla_mixer(h, cos, sin, mla_w_a[j], mla_q_a_gain[j], mla_kv_a_gain[j], mla_w_qb[j],
                          mla_w_kvb[j], mla_q_gain[j], mla_k_gain[j], mla_w_o[j])
        else:
            y = ssd_mixer(h, ssd_w_in[j], ssd_conv_w[j], ssd_conv_b[j], ssd_dt_bias[j], ssd_a_log[j],
                          ssd_d[j], ssd_norm_gain[j], ssd_w_out[j])
        x = x + m[:, None, 1, 2] * y
        h = modulate(x, norm_gain[i, 2], m[:, 2])
        x = x + 0.5 * m[:, None, 2, 2] * swiglu(h, ffn_w_gu[i, 1], ffn_w_down[i, 1])
    return x
```

```python
import functools
import math

import jax
import jax.numpy as jnp
from jax import lax
from jax.experimental import pallas as pl
from jax.experimental.pallas import tpu as pltpu

EPS = 1e-6
F32 = jnp.float32
BF16 = jnp.bfloat16

N_SUB = 3
MLA_HEADS = 16
Q_LORA = 384
KV_LORA = 256
QK_NOPE = 64
QK_ROPE = 32
QK_HEAD = QK_NOPE + QK_ROPE
V_HEAD = 64
ROPE_THETA = 10000.0
SSD_HEAD_DIM = 64
SSD_GROUPS = 4
SSD_STATE = 128
CONV_WIDTH = 4
CHUNK = 128

LANES = 128
SUBLANES = 8
QK_PAD = 128
VMEM_LIMIT_BYTES = 56 * 1024 * 1024

FFN_TM = 256
MLA_TM = 512
ATT_TQ = 512
ATT_TK = 512
OUT_TM = 512
SSD_TL = 256
MODS_TN = 1152


def _cparams(*sem):
    return pltpu.CompilerParams(dimension_semantics=sem, vmem_limit_bytes=VMEM_LIMIT_BYTES)


def _resident(shape):
    nd = len(shape)
    return pl.BlockSpec(shape, lambda *_: (0,) * nd, pipeline_mode=pl.Buffered(1))


def _modulate(x, gain, shift, scale):
    ms = jnp.mean(x * x, axis=-1, keepdims=True)
    y = x * lax.rsqrt(ms + EPS)
    return (y * gain) * (1.0 + scale) + shift


def _silu(v):
    return v * jax.nn.sigmoid(v)


def _mods_kernel(c_ref, w_ref, b_ref, o_ref):
    sc = _silu(c_ref[...]).astype(BF16)
    o_ref[...] = jnp.dot(sc, w_ref[...].astype(BF16), preferred_element_type=F32) + b_ref[...]


def _mods(c, ada_w, ada_b):
    depth, d, n = ada_w.shape
    bn = c.shape[0]
    bp = -(-bn // SUBLANES) * SUBLANES
    cp = jnp.zeros((bp, d), F32).at[:bn].set(c)
    out = pl.pallas_call(
        _mods_kernel,
        grid=(depth, n // MODS_TN),
        in_specs=[
            pl.BlockSpec((bp, d), lambda l, j: (0, 0)),
            pl.BlockSpec((None, d, MODS_TN), lambda l, j: (l, 0, j)),
            pl.BlockSpec((None, 1, MODS_TN), lambda l, j: (l, 0, j)),
        ],
        out_specs=pl.BlockSpec((None, bp, MODS_TN), lambda l, j: (l, 0, j)),
        out_shape=jax.ShapeDtypeStruct((depth, bp, n), F32),
        compiler_params=_cparams("parallel", "parallel"),
        name="adaln_mods",
    )(cp, ada_w, ada_b.reshape(depth, 1, n))
    return out[:, :bn].reshape(depth, bn, N_SUB, 3, d)


def _ffn_kernel(x_ref, mod_ref, gain_ref, wgu_ref, wd_ref, o_ref, *, d_ff):
    x = x_ref[...]
    m = mod_ref[...]
    h = _modulate(x, gain_ref[...], m[0:1], m[1:2]).astype(BF16)
    gu = jnp.dot(h, wgu_ref[...], preferred_element_type=F32)
    a = (_silu(gu[:, :d_ff]) * gu[:, d_ff:]).astype(BF16)
    y = jnp.dot(a, wd_ref[...], preferred_element_type=F32)
    o_ref[...] = x + (0.5 * m[2:3]) * y


def _ffn(x, mod, gain, w_gu, w_down):
    bn, s, d = x.shape
    d_ff = w_down.shape[0]
    return pl.pallas_call(
        functools.partial(_ffn_kernel, d_ff=d_ff),
        grid=(bn, s // FFN_TM),
        in_specs=[
            pl.BlockSpec((None, FFN_TM, d), lambda b, i: (b, i, 0)),
            pl.BlockSpec((None, 3, d), lambda b, i: (b, 0, 0)),
            _resident((1, d)),
            _resident(w_gu.shape),
            _resident(w_down.shape),
        ],
        out_specs=pl.BlockSpec((None, FFN_TM, d), lambda b, i: (b, i, 0)),
        out_shape=jax.ShapeDtypeStruct(x.shape, F32),
        compiler_params=_cparams("parallel", "parallel"),
        name="ffn",
    )(x, mod, gain.reshape(1, d), w_gu, w_down)


def _rope_t(v, cos, sin):
    half = QK_ROPE // 2
    v1, v2 = v[:half], v[half:]
    return jnp.concatenate([v1 * cos - v2 * sin, v1 * sin + v2 * cos], axis=0)


def _mla_proj_kernel(x_ref, mod_ref, gain_ref, pos_ref, inv_ref, wa_ref, qag_ref, kvag_ref,
                     wqb_ref, wkvb_ref, qg_ref, kg_ref, q_ref, k_ref, v_ref):
    m = mod_ref[...]
    h = _modulate(x_ref[...], gain_ref[...], m[0:1], m[1:2]).astype(BF16)
    nt = (((1,), (1,)), ((), ()))
    a_t = lax.dot_general(wa_ref[...], h, nt, preferred_element_type=F32)
    ql = a_t[:Q_LORA]
    kvl = a_t[Q_LORA:Q_LORA + KV_LORA]
    kr = a_t[Q_LORA + KV_LORA:]
    qn = ql * lax.rsqrt(jnp.mean(ql * ql, axis=0, keepdims=True) + EPS) * qag_ref[...]
    kvn = kvl * lax.rsqrt(jnp.mean(kvl * kvl, axis=0, keepdims=True) + EPS) * kvag_ref[...]
    q_all = jnp.dot(wqb_ref[...], qn.astype(BF16), preferred_element_type=F32)
    kv_all = jnp.dot(wkvb_ref[...], kvn.astype(BF16), preferred_element_type=F32)

    ang = pos_ref[...].astype(F32) * inv_ref[...]
    cos, sin = jnp.cos(ang), jnp.sin(ang)
    qg = qg_ref[...]
    kg = kg_ref[...]
    t = h.shape[0]
    zpad = jnp.zeros((QK_PAD - QK_HEAD, t), F32)
    kr_ss = jnp.sum(kr * kr, axis=0, keepdims=True)
    scale = QK_HEAD ** -0.5
    for hd in range(MLA_HEADS):
        qh = q_all[hd * QK_HEAD:(hd + 1) * QK_HEAD]
        qh = qh * lax.rsqrt(jnp.mean(qh * qh, axis=0, keepdims=True) + EPS) * qg
        qh = jnp.concatenate([qh[:QK_NOPE], _rope_t(qh[QK_NOPE:], cos, sin), zpad], axis=0)
        q_ref[hd] = (qh * scale).astype(BF16)

        base = hd * (QK_NOPE + V_HEAD)
        kn = kv_all[base:base + QK_NOPE]
        ss = jnp.sum(kn * kn, axis=0, keepdims=True) + kr_ss
        r = lax.rsqrt(ss * (1.0 / QK_HEAD) + EPS)
        kn = kn * r * kg[:QK_NOPE]
        krh = _rope_t(kr * r * kg[QK_NOPE:], cos, sin)
        kt = jnp.concatenate([kn, krh, zpad], axis=0)
        k_ref[hd] = kt.T.astype(BF16)
        v_ref[hd] = kv_all[base + QK_NOPE:base + QK_NOPE + V_HEAD].astype(BF16)


def _mla_proj(x, mod, gain, positions, inv_freq, wa_t, qa_gain, kva_gain, wqb_t, wkvb_t, q_gain, k_gain):
    bn, s, d = x.shape
    nt = s // MLA_TM
    hq = MLA_HEADS
    return pl.pallas_call(
        _mla_proj_kernel,
        grid=(bn, nt),
        in_specs=[
            pl.BlockSpec((None, MLA_TM, d), lambda b, i: (b, i, 0)),
            pl.BlockSpec((None, 3, d), lambda b, i: (b, 0, 0)),
            _resident((1, d)),
            pl.BlockSpec((None, 1, MLA_TM), lambda b, i: (b, 0, i)),
            _resident(inv_freq.shape),
            _resident(wa_t.shape),
            _resident((Q_LORA, 1)),
            _resident((KV_LORA, 1)),
            _resident(wqb_t.shape),
            _resident(wkvb_t.shape),
            _resident((QK_HEAD, 1)),
            _resident((QK_HEAD, 1)),
        ],
        out_specs=[
            pl.BlockSpec((None, hq, QK_PAD, MLA_TM), lambda b, i: (b, 0, 0, i)),
            pl.BlockSpec((None, hq, MLA_TM, QK_PAD), lambda b, i: (b, 0, i, 0)),
            pl.BlockSpec((None, hq, None, V_HEAD, MLA_TM), lambda b, i: (b, 0, i, 0, 0)),
        ],
        out_shape=[
            jax.ShapeDtypeStruct((bn, hq, QK_PAD, s), BF16),
            jax.ShapeDtypeStruct((bn, hq, s, QK_PAD), BF16),
            jax.ShapeDtypeStruct((bn, hq, nt, V_HEAD, MLA_TM), BF16),
        ],
        compiler_params=_cparams("parallel", "parallel"),
        name="mla_proj",
    )(x, mod, gain.reshape(1, d), positions.reshape(bn, 1, s), inv_freq, wa_t,
      qa_gain.reshape(Q_LORA, 1), kva_gain.reshape(KV_LORA, 1), wqb_t, wkvb_t,
      q_gain.reshape(QK_HEAD, 1), k_gain.reshape(QK_HEAD, 1))


def _attn_kernel(q_ref, k_ref, v_ref, o_ref, *, seq):
    nq = seq // ATT_TQ
    sub = ATT_TQ // ATT_TK
    krow = lax.broadcasted_iota(jnp.int32, (ATT_TK, ATT_TQ), 0)
    qcol = lax.broadcasted_iota(jnp.int32, (ATT_TK, ATT_TQ), 1)

    def step(carry, q_blk, kb, vb, mask):
        m, l, acc = carry
        s = jnp.dot(kb, q_blk, preferred_element_type=F32)
        if mask is not None:
            s = jnp.where(mask, s, -jnp.inf)
        m_new = jnp.maximum(m, jnp.max(s, axis=0, keepdims=True))
        p = jnp.exp(s - m_new)
        alpha = jnp.exp(m - m_new)
        l = alpha * l + jnp.sum(p, axis=0, keepdims=True)
        acc = alpha * acc + jnp.dot(vb, p.astype(BF16), preferred_element_type=F32)
        return m_new, l, acc

    for qi in range(nq):
        outs = []
        for hh in range(2):
            q_blk = q_ref[hh, :, qi * ATT_TQ:(qi + 1) * ATT_TQ]
            init = (jnp.full((1, ATT_TQ), -jnp.inf, F32), jnp.zeros((1, ATT_TQ), F32),
                    jnp.zeros((V_HEAD, ATT_TQ), F32))

            def body(j, carry, hh=hh, q_blk=q_blk):
                kb = k_ref[hh, pl.ds(pl.multiple_of(j * ATT_TK, ATT_TK), ATT_TK), :]
                vb = v_ref[hh, j]
                return step(carry, q_blk, kb, vb, None)

            carry = lax.fori_loop(0, qi * sub, body, init)
            for jj in range(sub):
                j = qi * sub + jj
                kb = k_ref[hh, j * ATT_TK:(j + 1) * ATT_TK, :]
                vb = v_ref[hh, j]
                mask = (krow + jj * ATT_TK) <= qcol
                carry = step(carry, q_blk, kb, vb, mask)
            _, l, acc = carry
            outs.append(acc / l)
        o = jnp.concatenate(outs, axis=0)
        o_ref[qi * ATT_TQ:(qi + 1) * ATT_TQ, :] = o.T.astype(BF16)


def _attention(q_t, k, v_blk):
    bn, hq, _, s = q_t.shape
    nk = s // ATT_TK
    assert v_blk.shape == (bn, hq, nk, V_HEAD, ATT_TK)
    return pl.pallas_call(
        functools.partial(_attn_kernel, seq=s),
        grid=(bn, hq // 2),
        in_specs=[
            pl.BlockSpec((None, 2, QK_PAD, s), lambda b, g: (b, g, 0, 0)),
            pl.BlockSpec((None, 2, s, QK_PAD), lambda b, g: (b, g, 0, 0)),
            pl.BlockSpec((None, 2, nk, V_HEAD, ATT_TK), lambda b, g: (b, g, 0, 0, 0)),
        ],
        out_specs=pl.BlockSpec((None, s, 2 * V_HEAD), lambda b, g: (b, 0, g)),
        out_shape=jax.ShapeDtypeStruct((bn, s, hq * V_HEAD), BF16),
        compiler_params=_cparams("parallel", "parallel"),
        name="mla_attention",
    )(q_t, k, v_blk)


def _out_proj_kernel(x_ref, mod_ref, o_ref, wo_ref, y_ref):
    y = jnp.dot(o_ref[...], wo_ref[...], preferred_element_type=F32)
    y_ref[...] = x_ref[...] + mod_ref[...][2:3] * y


def _mla_out(x, mod, o, w_o):
    bn, s, d = x.shape
    return pl.pallas_call(
        _out_proj_kernel,
        grid=(bn, s // OUT_TM),
        in_specs=[
            pl.BlockSpec((None, OUT_TM, d), lambda b, i: (b, i, 0)),
            pl.BlockSpec((None, 3, d), lambda b, i: (b, 0, 0)),
            pl.BlockSpec((None, OUT_TM, o.shape[-1]), lambda b, i: (b, i, 0)),
            _resident(w_o.shape),
        ],
        out_specs=pl.BlockSpec((None, OUT_TM, d), lambda b, i: (b, i, 0)),
        out_shape=jax.ShapeDtypeStruct(x.shape, F32),
        compiler_params=_cparams("parallel", "parallel"),
        name="mla_out",
    )(x, mod, o, w_o)


def _ssd_kernel(x_ref, mod_ref, gain_ref, wz_ref, wxbc_ref, wdt_ref, convw_ref, convb_ref,
                dtb_ref, alog_ref, dskip_ref, ng_ref, wout_ref, o_ref,
                tail_ref, state_ref, gn_ref, *, d_inner, n_heads):
    tl = x_ref.shape[0]
    gsz = d_inner // SSD_GROUPS
    hpg = n_heads // SSD_GROUPS
    bc = SSD_GROUPS * SSD_STATE

    @pl.when(pl.program_id(1) == 0)
    def _():
        tail_ref[...] = jnp.zeros_like(tail_ref)
        state_ref[...] = jnp.zeros_like(state_ref)

    x = x_ref[...]
    m = mod_ref[...]
    h = _modulate(x, gain_ref[...], m[0:1], m[1:2]).astype(BF16)
    z = jnp.dot(h, wz_ref[...], preferred_element_type=F32)
    xbc_raw = jnp.dot(h, wxbc_ref[...], preferred_element_type=F32)
    dt_raw = jnp.dot(h, wdt_ref[...], preferred_element_type=F32)

    ext = jnp.concatenate([tail_ref[...], xbc_raw], axis=0)
    tail_ref[...] = xbc_raw[tl - SUBLANES:]
    cw = convw_ref[...]
    acc = convb_ref[...] + cw[CONV_WIDTH - 1:CONV_WIDTH] * xbc_raw
    for sh in range(1, CONV_WIDTH):
        shifted = pltpu.roll(ext, sh, axis=0)[SUBLANES:]
        acc = acc + cw[CONV_WIDTH - 1 - sh:CONV_WIDTH - sh] * shifted
    xbc = _silu(acc)
    xs = xbc[:, :d_inner]
    bm = xbc[:, d_inner:d_inner + bc]
    cm = xbc[:, d_inner + bc:]

    dtv = dt_raw + dtb_ref[...]
    dt = jnp.maximum(dtv, 0.0) + jnp.log1p(jnp.exp(-jnp.abs(dtv)))
    a = dt * (-jnp.exp(alog_ref[...]))

    ri = lax.broadcasted_iota(jnp.int32, (CHUNK, CHUNK), 0)
    ci = lax.broadcasted_iota(jnp.int32, (CHUNK, CHUNK), 1)
    causal = ri >= ci
    tri = causal.astype(F32)
    lane_lo = ci < SSD_HEAD_DIM
    lane_lo_row = lane_lo[0:1]
    nt = (((1,), (1,)), ((), ()))

    for c in range(tl // CHUNK):
        r0 = c * CHUNK
        a_c = a[r0:r0 + CHUNK]
        dt_c = dt[r0:r0 + CHUNK]
        a_cum = jnp.dot(tri, a_c, precision=lax.Precision.HIGHEST, preferred_element_type=F32)
        a_cum_t = a_cum.T
        a_last = a_cum[CHUNK - 1:CHUNK]
        ys = []
        for g in range(SSD_GROUPS):
            b_g = bm[r0:r0 + CHUNK, g * SSD_STATE:(g + 1) * SSD_STATE]
            c_g = cm[r0:r0 + CHUNK, g * SSD_STATE:(g + 1) * SSD_STATE].astype(BF16)
            cb = lax.dot_general(c_g, b_g.astype(BF16), nt, preferred_element_type=F32)
            b_gt = b_g.T.astype(BF16)
            yd, xd, ecol, cdec = [], [], [], []
            for pr in range(hpg // 2):
                h1 = g * hpg + 2 * pr
                h2 = h1 + 1
                col1 = jnp.broadcast_to(a_cum[:, h1:h1 + 1], (CHUNK, CHUNK))
                col2 = jnp.broadcast_to(a_cum[:, h2:h2 + 1], (CHUNK, CHUNK))
                d1 = jnp.exp(jnp.where(causal, col1 - a_cum_t[h1:h1 + 1], -jnp.inf))
                d2 = jnp.exp(jnp.where(causal, col2 - a_cum_t[h2:h2 + 1], -jnp.inf))
                colp = jnp.where(lane_lo, col1, col2)
                dtp = jnp.where(lane_lo, jnp.broadcast_to(dt_c[:, h1:h1 + 1], (CHUNK, CHUNK)),
                                jnp.broadcast_to(dt_c[:, h2:h2 + 1], (CHUNK, CHUNK)))
                xdt = xs[r0:r0 + CHUNK, h1 * SSD_HEAD_DIM:(h2 + 1) * SSD_HEAD_DIM] * dtp
                xdt_b = xdt.astype(BF16)
                y1 = jnp.dot((cb * d1).astype(BF16), xdt_b, preferred_element_type=F32)
                y2 = jnp.dot((cb * d2).astype(BF16), xdt_b, preferred_element_type=F32)
                yd.append(jnp.where(lane_lo, y1, y2))
                lastp = jnp.where(lane_lo_row, a_last[:, h1:h1 + 1], a_last[:, h2:h2 + 1])
                xd.append((xdt * jnp.exp(lastp - colp)).astype(BF16))
                ecol.append(jnp.exp(colp))
                cdec.append(jnp.exp(lastp))
            xd_g = jnp.concatenate(xd, axis=1)
            new_t = jnp.dot(b_gt, xd_g, preferred_element_type=F32)
            prev_t = state_ref[g]
            y_off = jnp.dot(c_g, prev_t.astype(BF16), preferred_element_type=F32) * jnp.concatenate(ecol, axis=1)
            state_ref[g] = prev_t * jnp.concatenate(cdec, axis=1) + new_t
            ys.append(jnp.concatenate(yd, axis=1) + y_off)
        xs_c = xs[r0:r0 + CHUNK]
        z_c = z[r0:r0 + CHUNK]
        for g in range(SSD_GROUPS):
            sl = slice(g * gsz, (g + 1) * gsz)
            y_g = ys[g] + dskip_ref[:, sl] * xs_c[:, sl]
            gt = y_g * _silu(z_c[:, sl])
            gnorm = gt * lax.rsqrt(jnp.mean(gt * gt, axis=-1, keepdims=True) + EPS) * ng_ref[:, sl]
            gn_ref[r0:r0 + CHUNK, sl] = gnorm.astype(BF16)

    out = jnp.dot(gn_ref[...], wout_ref[...], preferred_element_type=F32)
    o_ref[...] = x + m[2:3] * out


def _ssd(x, mod, gain, w_z, w_xbc, w_dt, conv_w, conv_b, dt_bias, a_log, d_skip, norm_gain, w_out):
    bn, s, d = x.shape
    d_inner = w_z.shape[1]
    conv_dim = w_xbc.shape[1]
    n_heads = d_inner // SSD_HEAD_DIM
    return pl.pallas_call(
        functools.partial(_ssd_kernel, d_inner=d_inner, n_heads=n_heads),
        grid=(bn, s // SSD_TL),
        in_specs=[
            pl.BlockSpec((None, SSD_TL, d), lambda b, i: (b, i, 0)),
            pl.BlockSpec((None, 3, d), lambda b, i: (b, 0, 0)),
            _resident((1, d)),
            _resident(w_z.shape),
            _resident(w_xbc.shape),
            _resident(w_dt.shape),
            _resident(conv_w.shape),
            _resident((1, conv_dim)),
            _resident((1, LANES)),
            _resident((1, LANES)),
            _resident((1, d_inner)),
            _resident((1, d_inner)),
            _resident(w_out.shape),
        ],
        out_specs=pl.BlockSpec((None, SSD_TL, d), lambda b, i: (b, i, 0)),
        out_shape=jax.ShapeDtypeStruct(x.shape, F32),
        scratch_shapes=[
            pltpu.VMEM((SUBLANES, conv_dim), F32),
            pltpu.VMEM((SSD_GROUPS, SSD_STATE, d_inner // SSD_GROUPS), F32),
            pltpu.VMEM((SSD_TL, d_inner), BF16),
        ],
        compiler_params=_cparams("parallel", "arbitrary"),
        name="ssd_mixer",
    )(x, mod, gain.reshape(1, d), w_z, w_xbc, w_dt, conv_w, conv_b.reshape(1, conv_dim),
      dt_bias, a_log, d_skip, norm_gain.reshape(1, d_inner), w_out)


def _pad_lanes(v):
    return jnp.zeros((1, LANES), F32).at[0, :v.shape[0]].set(v)


def kernel(x, c, positions, norm_gain, ada_w, ada_b, ffn_w_gu, ffn_w_down, mla_w_a, mla_q_a_gain,
           mla_kv_a_gain, mla_w_qb, mla_w_kvb, mla_q_gain, mla_k_gain, mla_w_o, ssd_w_in, ssd_conv_w,
           ssd_conv_b, ssd_dt_bias, ssd_a_log, ssd_d, ssd_norm_gain, ssd_w_out):
    depth = norm_gain.shape[0]
    d_inner = ssd_w_out.shape[1]
    conv_dim = ssd_conv_w.shape[-1]
    n_heads = ssd_dt_bias.shape[-1]

    mods = _mods(c, ada_w, ada_b)
    inv_freq = (1.0 / (ROPE_THETA ** (jnp.arange(0, QK_ROPE, 2, dtype=F32) / QK_ROPE))).reshape(-1, 1)

    w_gu = ffn_w_gu.astype(BF16)
    w_down = ffn_w_down.astype(BF16)
    wa_t = jnp.swapaxes(mla_w_a, 1, 2).astype(BF16)
    wqb_t = jnp.swapaxes(mla_w_qb, 1, 2).astype(BF16)
    wkvb_t = jnp.swapaxes(mla_w_kvb, 1, 2).astype(BF16)
    w_o = mla_w_o.astype(BF16)
    w_in = ssd_w_in.astype(BF16)
    w_z = w_in[:, :, :d_inner]
    w_xbc = w_in[:, :, d_inner:d_inner + conv_dim]
    w_dt = jnp.zeros(w_in.shape[:2] + (LANES,), BF16).at[:, :, :n_heads].set(w_in[:, :, d_inner + conv_dim:])
    w_out = ssd_w_out.astype(BF16)

    for i in range(depth):
        j = i // 2
        m = mods[i]
        x = _ffn(x, m[:, 0], norm_gain[i, 0], w_gu[i, 0], w_down[i, 0])
        if i % 2 == 0:
            q_t, k, v_t = _mla_proj(x, m[:, 1], norm_gain[i, 1], positions, inv_freq, wa_t[j],
                                    mla_q_a_gain[j], mla_kv_a_gain[j], wqb_t[j], wkvb_t[j],
                                    mla_q_gain[j], mla_k_gain[j])
            o = _attention(q_t, k, v_t)
            x = _mla_out(x, m[:, 1], o, w_o[j])
        else:
            x = _ssd(x, m[:, 1], norm_gain[i, 1], w_z[j], w_xbc[j], w_dt[j], ssd_conv_w[j], ssd_conv_b[j],
                     _pad_lanes(ssd_dt_bias[j]), _pad_lanes(ssd_a_log[j]),
                     jnp.repeat(ssd_d[j], SSD_HEAD_DIM).reshape(1, d_inner), ssd_norm_gain[j], w_out[j])
        x = _ffn(x, m[:, 2], norm_gain[i, 2], w_gu[i, 1], w_down[i, 1])
    return x
```
